```python
import jax, jax.numpy as jnp
from jax import lax
import numpy as np

D_MODEL = 1024
BATCH = 4
SEQ = 8192
DEPTH = 2

GROUP_WIDTH = D_MODEL // 2
D_MIX = 3 * GROUP_WIDTH
BLOCK = 128
RMS_EPS = 1e-6
NEG_INF = -1e30

MLA_HEADS = 8
MLA_NOPE_DIM = 64
MLA_ROPE_DIM = 32
MLA_QK_DIM = MLA_NOPE_DIM + MLA_ROPE_DIM
MLA_V_DIM = 64
MLA_Q_LORA = 256
MLA_KV_LORA = 128
ROPE_THETA = 10000.0

CONV_DIM = GROUP_WIDTH
CONV_WIDTH = 3

SWA_HEADS = 8
SWA_KV_HEADS = 2
SWA_GROUP = SWA_HEADS // SWA_KV_HEADS
SWA_HEAD_DIM = 64
SWA_WINDOW = 128

IN_SPLITS = (
    MLA_Q_LORA, MLA_KV_LORA, MLA_ROPE_DIM, GROUP_WIDTH,
    CONV_DIM, CONV_DIM, CONV_DIM, GROUP_WIDTH,
    SWA_HEADS * SWA_HEAD_DIM, SWA_KV_HEADS * SWA_HEAD_DIM,
    SWA_KV_HEADS * SWA_HEAD_DIM, GROUP_WIDTH,
)
IN_COLS = sum(IN_SPLITS)

kernel_name = "hymba_mla_shortconv_swa_hybrid"


def rms_norm(x, g):
    xf = x.astype(jnp.float32)
    y = xf * lax.rsqrt(jnp.mean(xf * xf, axis=-1, keepdims=True) + RMS_EPS)
    return (y * g.astype(jnp.float32)).astype(x.dtype)


def apply_rope(x, pos):
    half = x.shape[-1] // 2
    inv_freq = jnp.power(jnp.float32(ROPE_THETA), -jnp.arange(half, dtype=jnp.float32) / half)
    ang = pos[:, None] * inv_freq[None, :]
    cos = jnp.cos(ang)[:, None, :]
    sin = jnp.sin(ang)[:, None, :]
    xf = x.astype(jnp.float32)
    x1, x2 = xf[..., :half], xf[..., half:]
    out = jnp.concatenate([x1 * cos - x2 * sin, x2 * cos + x1 * sin], axis=-1)
    return out.astype(x.dtype)


def mla_mixer(q_lat, kv_lat, k_rope, q_a_norm, w_qb, kv_a_norm, w_kvb, q_norm, k_norm):
    b, s, _ = q_lat.shape
    q = (rms_norm(q_lat, q_a_norm) @ w_qb).reshape(b, s, MLA_HEADS, MLA_QK_DIM)
    kv = (rms_norm(kv_lat, kv_a_norm) @ w_kvb).reshape(b, s, MLA_HEADS, MLA_NOPE_DIM + MLA_V_DIM)
    k_nope, v = kv[..., :MLA_NOPE_DIM], kv[..., MLA_NOPE_DIM:]
    k_pe = jnp.broadcast_to(k_rope[:, :, None, :], (b, s, MLA_HEADS, MLA_ROPE_DIM))
    k = jnp.concatenate([k_nope, k_pe], axis=-1)
    q = rms_norm(q, q_norm)
    k = rms_norm(k, k_norm)
    pos = jnp.arange(s, dtype=jnp.float32)
    q = jnp.concatenate([q[..., :MLA_NOPE_DIM], apply_rope(q[..., MLA_NOPE_DIM:], pos)], axis=-1)
    k = jnp.concatenate([k[..., :MLA_NOPE_DIM], apply_rope(k[..., MLA_NOPE_DIM:], pos)], axis=-1)
    scale = MLA_QK_DIM ** -0.5
    nb = s // BLOCK
    q_blocks = q.reshape(b, nb, BLOCK, MLA_HEADS, MLA_QK_DIM).transpose(1, 0, 2, 3, 4)
    starts = jnp.arange(nb, dtype=jnp.int32) * BLOCK
    key_pos = jnp.arange(s, dtype=jnp.int32)

    def attend_block(args):
        qb, start = args
        sc = jnp.einsum('bqhd,bkhd->bhqk', qb, k, preferred_element_type=jnp.float32) * scale
        q_pos = start + jnp.arange(BLOCK, dtype=jnp.int32)
        causal = key_pos[None, :] <= q_pos[:, None]
        sc = jnp.where(causal[None, None], sc, NEG_INF)
        p = jax.nn.softmax(sc, axis=-1).astype(v.dtype)
        return jnp.einsum('bhqk,bkhd->bqhd', p, v)

    o = lax.map(attend_block, (q_blocks, starts))
    return o.transpose(1, 0, 2, 3, 4).reshape(b, s, MLA_HEADS * MLA_V_DIM)


def short_conv_mixer(h, b_gate, c_gate, conv_w):
    u = c_gate * h
    y = lax.conv_general_dilated(
        u, conv_w[:, None, :].astype(u.dtype), window_strides=(1,),
        padding=[(CONV_WIDTH - 1, 0)], dimension_numbers=('NWC', 'WIO', 'NWC'),
        feature_group_count=CONV_DIM)
    return b_gate * y


def _band(t):
    b, s, kv, d = t.shape
    nb = s // BLOCK
    tp = jnp.concatenate([jnp.zeros((b, BLOCK, kv, d), t.dtype), t], axis=1)
    tp = tp.reshape(b, nb + 1, BLOCK, kv, d)
    return jnp.concatenate([tp[:, :-1], tp[:, 1:]], axis=2)


def swa_mixer(q, k, v, q_norm, k_norm, sinks):
    b, s, _ = q.shape
    nb = s // BLOCK
    q = rms_norm(q.reshape(b, s, SWA_HEADS, SWA_HEAD_DIM), q_norm)
    k = rms_norm(k.reshape(b, s, SWA_KV_HEADS, SWA_HEAD_DIM), k_norm)
    v = v.reshape(b, s, SWA_KV_HEADS, SWA_HEAD_DIM)
    qb = q.reshape(b, nb, BLOCK, SWA_KV_HEADS, SWA_GROUP, SWA_HEAD_DIM)
    kb, vb = _band(k), _band(v)
    sc = jnp.einsum('bnqkgd,bnskd->bnkgqs', qb, kb,
                    preferred_element_type=jnp.float32) * (SWA_HEAD_DIM ** -0.5)
    q_idx = jnp.arange(BLOCK, dtype=jnp.int32)[:, None]
    k_idx = jnp.arange(2 * BLOCK, dtype=jnp.int32)[None, :]
    dist = BLOCK + q_idx - k_idx
    key_pos = (jnp.arange(nb, dtype=jnp.int32)[:, None] - 1) * BLOCK + k_idx
    valid = ((dist >= 0) & (dist < SWA_WINDOW))[None] & (key_pos >= 0)[:, None, :]
    slopes = jnp.exp2(-8.0 * jnp.arange(1, SWA_HEADS + 1, dtype=jnp.float32) / SWA_HEADS)
    slopes = slopes.reshape(SWA_KV_HEADS, SWA_GROUP)
    sc = sc - slopes[None, None, :, :, None, None] * dist.astype(jnp.float32)[None, None, None, None]
    sc = jnp.where(valid[None, :, None, None], sc, NEG_INF)
    sink = jnp.broadcast_to(
        sinks.astype(jnp.float32).reshape(SWA_KV_HEADS, SWA_GROUP)[None, None, :, :, None, None],
        sc.shape[:-1] + (1,))
    p = jax.nn.softmax(jnp.concatenate([sc, sink], axis=-1), axis=-1)[..., :-1].astype(v.dtype)
    o = jnp.einsum('bnkgqs,bnskd->bnqkgd', p, vb)
    return o.reshape(b, s, SWA_HEADS * SWA_HEAD_DIM)


def hybrid_layer(x, norm_g, w_in, mla_q_a_norm, mla_w_qb, mla_kv_a_norm, mla_w_kvb,
                 mla_q_norm, mla_k_norm, conv_w, swa_q_norm, swa_k_norm, swa_sinks, w_out):
    h = rms_norm(x, norm_g)
    proj = h @ w_in
    offsets = [int(o) for o in np.cumsum(IN_SPLITS)[:-1]]
    (q_lat, kv_lat, k_rope, g_mla,
     c_h, c_b, c_c, g_conv,
     s_q, s_k, s_v, g_swa) = jnp.split(proj, offsets, axis=-1)
    y_mla = mla_mixer(q_lat, kv_lat, k_rope, mla_q_a_norm, mla_w_qb, mla_kv_a_norm,
                      mla_w_kvb, mla_q_norm, mla_k_norm) * jax.nn.silu(g_mla)
    y_conv = short_conv_mixer(c_h, c_b, c_c, conv_w) * jax.nn.silu(g_conv)
    y_swa = swa_mixer(s_q, s_k, s_v, swa_q_norm, swa_k_norm, swa_sinks) * jax.nn.silu(g_swa)
    y = jnp.concatenate([y_mla, y_conv, y_swa], axis=-1) @ w_out
    return x + y


def setup_inputs(seed: int = 0) -> dict:
    key = jax.random.key(seed)
    ks = jax.random.split(key, 14)
    f32 = jnp.float32

    def nrm(k, shape, scale):
        return jax.random.normal(k, shape, f32) * scale

    def gain(k, n):
        return 1.0 + 0.02 * jax.random.normal(k, (DEPTH, n), f32)

    return {
        "x": jax.random.normal(ks[0], (BATCH, SEQ, D_MODEL), f32),
        "norm_g": gain(ks[1], D_MODEL),
        "w_in": nrm(ks[2], (DEPTH, D_MODEL, IN_COLS), D_MODEL ** -0.5),
        "mla_q_a_norm": gain(ks[3], MLA_Q_LORA),
        "mla_w_qb": nrm(ks[4], (DEPTH, MLA_Q_LORA, MLA_HEADS * MLA_QK_DIM), MLA_Q_LORA ** -0.5),
        "mla_kv_a_norm": gain(ks[5], MLA_KV_LORA),
        "mla_w_kvb": nrm(ks[6], (DEPTH, MLA_KV_LORA, MLA_HEADS * (MLA_NOPE_DIM + MLA_V_DIM)), MLA_KV_LORA ** -0.5),
        "mla_q_norm": gain(ks[7], MLA_QK_DIM),
        "mla_k_norm": gain(ks[8], MLA_QK_DIM),
        "conv_w": nrm(ks[9], (DEPTH, CONV_WIDTH, CONV_DIM), CONV_WIDTH ** -0.5),
        "swa_q_norm": gain(ks[10], SWA_HEAD_DIM),
        "swa_k_norm": gain(ks[11], SWA_HEAD_DIM),
        "swa_sinks": nrm(ks[12], (DEPTH, SWA_HEADS), 0.5),
        "w_out": nrm(ks[13], (DEPTH, D_MIX, D_MODEL), D_MIX ** -0.5),
    }


def reference(x, norm_g, w_in, mla_q_a_norm, mla_w_qb, mla_kv_a_norm, mla_w_kvb,
              mla_q_norm, mla_k_norm, conv_w, swa_q_norm, swa_k_norm, swa_sinks, w_out):
    for l in range(DEPTH):
        x = hybrid_layer(x, norm_g[l], w_in[l], mla_q_a_norm[l], mla_w_qb[l], mla_kv_a_norm[l],
                         mla_w_kvb[l], mla_q_norm[l], mla_k_norm[l], conv_w[l], swa_q_norm[l],
                         swa_k_norm[l], swa_sinks[l], w_out[l])
    return x
```

```python
import functools
import math

import numpy as np
import jax
import jax.numpy as jnp
from jax import lax
from jax.experimental import pallas as pl
from jax.experimental.pallas import tpu as pltpu

F32 = jnp.float32
BF16 = jnp.bfloat16

D_MODEL = 1024
GROUP_WIDTH = 512
RMS_EPS = 1e-6
NEG_INF = -1e30
MLA_HEADS = 8
MLA_NOPE = 64
MLA_ROPE = 32
MLA_QK = MLA_NOPE + MLA_ROPE
MLA_V = 64
MLA_Q_LORA = 256
MLA_KV_LORA = 128
ROPE_THETA = 10000.0
CONV_WIDTH = 3
SWA_HEADS = 8
SWA_KV_HEADS = 2
SWA_GROUP = SWA_HEADS // SWA_KV_HEADS
SWA_D = 64
SWA_WINDOW = 128
IN_SPLITS = (256, 128, 32, 512, 512, 512, 512, 512, 512, 128, 128, 512)

LANES = 128
QK_PAD = 128
LOG2E = 1.4426950408889634

TM = 256
SUB = SWA_WINDOW
HALF_ROPE = MLA_ROPE // 2

_T_QLAT, _T_KVLAT, _T_KROPE, _T_SQ, _T_SK, _T_SV, _T_GMLA, _T_GSWA, _T_END = (
    0, 256, 384, 416, 928, 1056, 1184, 1696, 2208)


def _silu(g):
    return g * (1.0 / (1.0 + jnp.exp(-g)))


def _rope_rows(x1, x2, cos, sin):
    return jnp.concatenate([x1 * cos - x2 * sin, x2 * cos + x1 * sin], axis=0)


def _prep_kernel(x_ref, ng_ref, wstd_ref, wt_ref, wqb_ref, wkvb_ref, gq_ref, gk_ref, invf_ref,
                 convw_ref, sgq_ref, sgk_ref, bias_ref, sink_ref,
                 qt_ref, k_ref, vt_ref, gmla_ref, yconv_ref, yswa_ref,
                 ucarry_ref, kcarry_ref, vcarry_ref):
    i = pl.program_id(1)
    tm = x_ref.shape[0]

    @pl.when(i == 0)
    def _():
        ucarry_ref[...] = jnp.zeros_like(ucarry_ref)
        kcarry_ref[...] = jnp.zeros_like(kcarry_ref)
        vcarry_ref[...] = jnp.zeros_like(vcarry_ref)

    x = x_ref[...]
    r = lax.rsqrt(jnp.mean(x * x, axis=-1, keepdims=True) + RMS_EPS)
    xn = (x * r * ng_ref[...]).astype(BF16)

    ps = jnp.dot(xn, wstd_ref[...], preferred_element_type=F32)
    c_h, c_b = ps[:, 0:512], ps[:, 512:1024]
    c_c, g_c = ps[:, 1024:1536], ps[:, 1536:2048]
    u = c_c * c_h
    prev = ucarry_ref[...]
    row8 = lax.broadcasted_iota(jnp.int32, (8, GROUP_WIDTH), 0)

    def shifted(s):
        ru = pltpu.roll(u, s, 0)
        rp = pltpu.roll(prev, s, 0)
        head = jnp.where(row8 < s, rp, ru[0:8])
        return jnp.concatenate([head, ru[8:]], axis=0)

    w = convw_ref[...]
    y = c_b * (w[0:1] * shifted(2) + w[1:2] * shifted(1) + w[2:3] * u)
    yconv_ref[...] = (y * _silu(g_c)).astype(BF16)
    ucarry_ref[...] = u[tm - 8:tm]

    pt = lax.dot_general(wt_ref[...], xn, (((1,), (1,)), ((), ())),
                         preferred_element_type=F32)

    pos = (i * tm + lax.broadcasted_iota(jnp.int32, (HALF_ROPE, tm), 1)).astype(F32)
    ang = pos * invf_ref[...]
    cos, sin = jnp.cos(ang), jnp.sin(ang)

    ql = pt[_T_QLAT:_T_KVLAT]
    qn = (ql * lax.rsqrt(jnp.mean(ql * ql, axis=0, keepdims=True) + RMS_EPS)).astype(BF16)
    q_all = jnp.dot(wqb_ref[...], qn, preferred_element_type=F32)
    gq = gq_ref[...]
    for h in range(MLA_HEADS):
        qh = q_all[QK_PAD * h:QK_PAD * (h + 1)]
        rq = lax.rsqrt(jnp.sum(qh * qh, axis=0, keepdims=True) * (1.0 / MLA_QK) + RMS_EPS)
        qh = qh * rq * gq
        qh = jnp.concatenate(
            [qh[0:MLA_NOPE],
             _rope_rows(qh[MLA_NOPE:MLA_NOPE + HALF_ROPE], qh[MLA_NOPE + HALF_ROPE:MLA_QK], cos, sin),
             qh[MLA_QK:QK_PAD]], axis=0)
        qt_ref[h] = qh.astype(BF16)

    kvl = pt[_T_KVLAT:_T_KROPE]
    kvn = (kvl * lax.rsqrt(jnp.mean(kvl * kvl, axis=0, keepdims=True) + RMS_EPS)).astype(BF16)
    kv_all = jnp.dot(wkvb_ref[...], kvn, preferred_element_type=F32)
    gk = gk_ref[...]
    kpe = pt[_T_KROPE:_T_SQ]
    ss_pe = jnp.sum(kpe * kpe, axis=0, keepdims=True)
    kpg = kpe * gk[MLA_NOPE:MLA_QK]
    kpr = _rope_rows(kpg[0:HALF_ROPE], kpg[HALF_ROPE:MLA_ROPE], cos, sin)
    kpad = jnp.zeros((QK_PAD - MLA_QK, tm), F32)
    for h in range(MLA_HEADS):
        kn = kv_all[MLA_NOPE * h:MLA_NOPE * (h + 1)]
        rk = lax.rsqrt((jnp.sum(kn * kn, axis=0, keepdims=True) + ss_pe) * (1.0 / MLA_QK) + RMS_EPS)
        kh_t = jnp.concatenate([kn * gk[0:MLA_NOPE] * rk, kpr * rk, kpad], axis=0)
        k_ref[h] = kh_t.T.astype(BF16)
        v0 = MLA_HEADS * MLA_NOPE + MLA_V * h
        vt_ref[h] = kv_all[v0:v0 + MLA_V].astype(BF16)

    gmla_ref[...] = _silu(pt[_T_GMLA:_T_GSWA]).astype(BF16)

    sq = pt[_T_SQ:_T_SK]
    sgq = sgq_ref[...]
    sq_heads = []
    for h in range(SWA_HEADS):
        b = sq[SWA_D * h:SWA_D * (h + 1)]
        b = b * lax.rsqrt(jnp.mean(b * b, axis=0, keepdims=True) + RMS_EPS) * sgq
        sq_heads.append(b.astype(BF16))
    sk = pt[_T_SK:_T_SV]
    sgk = sgk_ref[...]
    sk_heads = []
    for h in range(SWA_KV_HEADS):
        b = sk[SWA_D * h:SWA_D * (h + 1)]
        sk_heads.append(b * lax.rsqrt(jnp.mean(b * b, axis=0, keepdims=True) + RMS_EPS) * sgk)
    k_std = jnp.concatenate(sk_heads, axis=0).T.astype(BF16)
    sv_t = pt[_T_SV:_T_GMLA].astype(BF16)
    gsw = _silu(pt[_T_GSWA:_T_END])
    zq = jnp.zeros((SWA_D, SWA_GROUP * SUB), BF16)

    for sb in range(tm // SUB):
        lanes = slice(SUB * sb, SUB * (sb + 1))
        if sb == 0:
            k_prev, v_prev = kcarry_ref[...], vcarry_ref[...]
            pen = jnp.where(i == 0, NEG_INF, 0.0).astype(F32)
        else:
            k_prev, v_prev = k_std[SUB * (sb - 1):SUB * sb], sv_t[:, SUB * (sb - 1):SUB * sb]
            pen = None
        k_band = jnp.concatenate([k_prev, k_std[lanes]], axis=0)
        v_band = jnp.concatenate([v_prev, sv_t[:, lanes]], axis=1)
        for kvh in range(SWA_KV_HEADS):
            heads = [SWA_GROUP * kvh + g for g in range(SWA_GROUP)]
            q4 = jnp.concatenate([sq_heads[h][:, lanes] for h in heads], axis=1)
            q_ext = jnp.concatenate([q4, zq] if kvh == 0 else [zq, q4], axis=0)
            s = jnp.dot(k_band, q_ext, preferred_element_type=F32) + bias_ref[kvh]
            if pen is not None:
                s = jnp.concatenate([s[0:SUB] + pen, s[SUB:2 * SUB]], axis=0)
            snk = sink_ref[kvh]
            m = jnp.maximum(jnp.max(s, axis=0, keepdims=True), snk)
            p = jnp.exp2(s - m)
            l = jnp.sum(p, axis=0, keepdims=True) + jnp.exp2(snk - m)
            o_t = jnp.dot(v_band[SWA_D * kvh:SWA_D * (kvh + 1)], p.astype(BF16),
                          preferred_element_type=F32) / l
            for g, h in enumerate(heads):
                yh = o_t[:, SUB * g:SUB * (g + 1)] * gsw[SWA_D * h:SWA_D * (h + 1), lanes]
                yswa_ref[SWA_D * h:SWA_D * (h + 1), lanes] = yh.astype(BF16)
    kcarry_ref[...] = k_std[tm - SUB:tm]
    vcarry_ref[...] = sv_t[:, tm - SUB:tm]


def _flash_kernel(q_ref, k_ref, vt_ref, g_ref, o_ref):
    nq, _, tq = q_ref.shape
    tk = tq
    row = lax.broadcasted_iota(jnp.int32, (tk, tq), 0)
    col = lax.broadcasted_iota(jnp.int32, (tk, tq), 1)
    diag_ok = row <= col

    def step(j, qt, carry, masked):
        m, l, acc = carry
        kj = k_ref[pl.ds(pl.multiple_of(j * tk, tk), tk), :]
        s = jnp.dot(kj, qt, preferred_element_type=F32)
        if masked:
            s = jnp.where(diag_ok, s, NEG_INF)
        m_new = jnp.maximum(m, jnp.max(s, axis=0, keepdims=True))
        alpha = jnp.exp2(m - m_new)
        p = jnp.exp2(s - m_new)
        l = alpha * l + jnp.sum(p, axis=0, keepdims=True)
        acc = alpha * acc + jnp.dot(vt_ref[j], p.astype(BF16), preferred_element_type=F32)
        return m_new, l, acc

    def q_block(i, _):
        qt = q_ref[i]
        init = (jnp.full((1, tq), NEG_INF, F32), jnp.zeros((1, tq), F32), jnp.zeros((MLA_V, tq), F32))
        carry = lax.fori_loop(0, i, lambda j, c: step(j, qt, c, False), init)
        m, l, acc = step(i, qt, carry, True)
        o_ref[i] = (acc / l * g_ref[i].astype(F32)).astype(o_ref.dtype)
        return 0

    lax.fori_loop(0, nq, q_block, 0)


def _out_kernel(x_ref, ymla_ref, yconv_ref, yswa_ref, wout_ref, o_ref):
    tn = (((0,), (0,)), ((), ()))
    acc = lax.dot_general(ymla_ref[...], wout_ref[0:GROUP_WIDTH], tn, preferred_element_type=F32)
    acc += jnp.dot(yconv_ref[...], wout_ref[GROUP_WIDTH:2 * GROUP_WIDTH], preferred_element_type=F32)
    acc += lax.dot_general(yswa_ref[...], wout_ref[2 * GROUP_WIDTH:3 * GROUP_WIDTH], tn,
                           preferred_element_type=F32)
    o_ref[...] = x_ref[...] + acc


def _lane_bcast(v, n):
    return jnp.broadcast_to(v.astype(F32)[:, None], (v.shape[0], n))


def _swa_bias():
    r = np.arange(2 * SUB)[:, None]
    c = np.arange(SUB)[None, :]
    dist = SUB + c - r
    valid = (dist >= 0) & (dist < SWA_WINDOW)
    slopes = np.exp2(-8.0 * np.arange(1, SWA_HEADS + 1, dtype=np.float32) / SWA_HEADS).astype(np.float32)
    out = np.empty((SWA_KV_HEADS, 2 * SUB, SWA_GROUP * SUB), np.float32)
    for h in range(SWA_HEADS):
        b = np.where(valid, -(slopes[h] * dist.astype(np.float32)) * np.float32(LOG2E), np.float32(NEG_INF))
        out[h // SWA_GROUP, :, SUB * (h % SWA_GROUP):SUB * (h % SWA_GROUP + 1)] = b
    return out


def _layer(x, norm_g, w_in, q_a_norm, w_qb, kv_a_norm, w_kvb, q_norm, k_norm, conv_w,
           swa_q_norm, swa_k_norm, swa_sinks, w_out):
    bsz, seq, _ = x.shape
    nq = seq // TM
    offs = np.cumsum((0,) + IN_SPLITS)
    col = lambda a, b: w_in[:, offs[a]:offs[b]]
    w_std = col(4, 8).astype(BF16)
    w_t = jnp.concatenate([col(0, 3), col(8, 11), col(3, 4), col(11, 12)], axis=1).T.astype(BF16)
    wqb = (q_a_norm[:, None] * w_qb).reshape(MLA_Q_LORA, MLA_HEADS, MLA_QK)
    wqb = jnp.pad(wqb, ((0, 0), (0, 0), (0, QK_PAD - MLA_QK))).reshape(MLA_Q_LORA, MLA_HEADS * QK_PAD)
    wqb_t = wqb.T.astype(BF16)
    wkvb = (kv_a_norm[:, None] * w_kvb).reshape(MLA_KV_LORA, MLA_HEADS, MLA_NOPE + MLA_V)
    wkvb = jnp.concatenate([wkvb[:, :, :MLA_NOPE].reshape(MLA_KV_LORA, -1),
                            wkvb[:, :, MLA_NOPE:].reshape(MLA_KV_LORA, -1)], axis=1)
    wkvb_t = wkvb.T.astype(BF16)
    pad = jnp.zeros((QK_PAD - MLA_QK,), F32)
    gq = _lane_bcast(jnp.concatenate([q_norm * (MLA_QK ** -0.5 * LOG2E), pad]), TM)
    gk = _lane_bcast(jnp.concatenate([k_norm, pad]), TM)
    half = HALF_ROPE
    inv_freq = jnp.power(jnp.float32(ROPE_THETA), -jnp.arange(half, dtype=F32) / half)
    invf = _lane_bcast(inv_freq, TM)
    sgq = _lane_bcast(swa_q_norm * (SWA_D ** -0.5 * LOG2E), TM)
    sgk = _lane_bcast(swa_k_norm, TM)
    bias = jnp.asarray(_swa_bias())
    sinks = jnp.repeat(swa_sinks.astype(F32) * LOG2E, SUB).reshape(SWA_KV_HEADS, 1, SWA_GROUP * SUB)

    const = lambda shape: pl.BlockSpec(shape, lambda b, i: (0,) * len(shape))
    n_t = _T_END
    qt, kk, vt, gmla, yconv, yswa = pl.pallas_call(
        _prep_kernel,
        grid=(bsz, nq),
        in_specs=[
            pl.BlockSpec((None, TM, D_MODEL), lambda b, i: (b, i, 0)),
            const((1, D_MODEL)), const((D_MODEL, 4 * GROUP_WIDTH)), const((n_t, D_MODEL)),
            const((MLA_HEADS * QK_PAD, MLA_Q_LORA)), const((2 * GROUP_WIDTH, MLA_KV_LORA)),
            const((QK_PAD, TM)), const((QK_PAD, TM)), const((HALF_ROPE, TM)),
            const((CONV_WIDTH, GROUP_WIDTH)), const((SWA_D, TM)), const((SWA_D, TM)),
            const((SWA_KV_HEADS, 2 * SUB, SWA_GROUP * SUB)), const((SWA_KV_HEADS, 1, SWA_GROUP * SUB)),
        ],
        out_specs=[
            pl.BlockSpec((None, MLA_HEADS, None, QK_PAD, TM), lambda b, i: (b, 0, i, 0, 0)),
            pl.BlockSpec((None, MLA_HEADS, TM, QK_PAD), lambda b, i: (b, 0, i, 0)),
            pl.BlockSpec((None, MLA_HEADS, None, MLA_V, TM), lambda b, i: (b, 0, i, 0, 0)),
            pl.BlockSpec((None, None, GROUP_WIDTH, TM), lambda b, i: (b, i, 0, 0)),
            pl.BlockSpec((None, TM, GROUP_WIDTH), lambda b, i: (b, i, 0)),
            pl.BlockSpec((None, None, GROUP_WIDTH, TM), lambda b, i: (b, i, 0, 0)),
        ],
        out_shape=[
            jax.ShapeDtypeStruct((bsz, MLA_HEADS, nq, QK_PAD, TM), BF16),
            jax.ShapeDtypeStruct((bsz, MLA_HEADS, seq, QK_PAD), BF16),
            jax.ShapeDtypeStruct((bsz, MLA_HEADS, nq, MLA_V, TM), BF16),
            jax.ShapeDtypeStruct((bsz, nq, GROUP_WIDTH, TM), BF16),
            jax.ShapeDtypeStruct((bsz, seq, GROUP_WIDTH), BF16),
            jax.ShapeDtypeStruct((bsz, nq, GROUP_WIDTH, TM), BF16),
        ],
        scratch_shapes=[pltpu.VMEM((8, GROUP_WIDTH), F32), pltpu.VMEM((SUB, 2 * SWA_D), BF16),
                        pltpu.VMEM((2 * SWA_D, SUB), BF16)],
        compiler_params=pltpu.CompilerParams(dimension_semantics=("arbitrary", "arbitrary"),
                                             vmem_limit_bytes=56 * 1024 * 1024),
        name="prep",
    )(x, norm_g.reshape(1, D_MODEL), w_std, w_t, wqb_t, wkvb_t, gq, gk, invf, conv_w,
      sgq, sgk, bias, sinks)

    ymla = pl.pallas_call(
        _flash_kernel,
        grid=(bsz, MLA_HEADS),
        in_specs=[
            pl.BlockSpec((None, None, nq, QK_PAD, TM), lambda b, h: (b, h, 0, 0, 0)),
            pl.BlockSpec((None, None, seq, QK_PAD), lambda b, h: (b, h, 0, 0)),
            pl.BlockSpec((None, None, nq, MLA_V, TM), lambda b, h: (b, h, 0, 0, 0)),
            pl.BlockSpec((None, nq, MLA_V, TM), lambda b, h: (b, 0, h, 0)),
        ],
        out_specs=pl.BlockSpec((None, nq, MLA_V, TM), lambda b, h: (b, 0, h, 0)),
        out_shape=jax.ShapeDtypeStruct((bsz, nq, GROUP_WIDTH, TM), BF16),
        compiler_params=pltpu.CompilerParams(dimension_semantics=("arbitrary", "arbitrary"),
                                             vmem_limit_bytes=40 * 1024 * 1024),
        name="flash",
    )(qt, kk, vt, gmla)

    return pl.pallas_call(
        _out_kernel,
        grid=(bsz, nq),
        in_specs=[
            pl.BlockSpec((None, TM, D_MODEL), lambda b, i: (b, i, 0)),
            pl.BlockSpec((None, None, GROUP_WIDTH, TM), lambda b, i: (b, i, 0, 0)),
            pl.BlockSpec((None, TM, GROUP_WIDTH), lambda b, i: (b, i, 0)),
            pl.BlockSpec((None, None, GROUP_WIDTH, TM), lambda b, i: (b, i, 0, 0)),
            pl.BlockSpec((3 * GROUP_WIDTH, D_MODEL), lambda b, i: (0, 0)),
        ],
        out_specs=pl.BlockSpec((None, TM, D_MODEL), lambda b, i: (b, i, 0)),
        out_shape=jax.ShapeDtypeStruct((bsz, seq, D_MODEL), F32),
        compiler_params=pltpu.CompilerParams(dimension_semantics=("arbitrary", "arbitrary"),
                                             vmem_limit_bytes=32 * 1024 * 1024),
        name="outproj",
    )(x, ymla, yconv, yswa, w_out.astype(BF16))


def kernel(x, norm_g, w_in, mla_q_a_norm, mla_w_qb, mla_kv_a_norm, mla_w_kvb, mla_q_norm, mla_k_norm,
           conv_w, swa_q_norm, swa_k_norm, swa_sinks, w_out):
    for l in range(norm_g.shape[0]):
        x = _layer(x, norm_g[l], w_in[l], mla_q_a_norm[l], mla_w_qb[l], mla_kv_a_norm[l], mla_w_kvb[l],
                   mla_q_norm[l], mla_k_norm[l], conv_w[l], swa_q_norm[l], swa_k_norm[l], swa_sinks[l],
                   w_out[l])
    return x
```
